```python
import math
import jax, jax.numpy as jnp
from jax import lax
import numpy as np

D_MODEL = 2048
BATCH = 4
SEQ = 4096
DEPTH = 1

MIX_W = D_MODEL
FOURIER_W = MIX_W // 2
N_FOURIER_GROUPS = 4
FOURIER_GROUP_W = FOURIER_W // N_FOURIER_GROUPS
ATTN_W = MIX_W - FOURIER_W
HEAD_DIM = 128
N_HEADS = ATTN_W // HEAD_DIM
ROPE_THETA = 500000.0
ROPE_DIM = HEAD_DIM // 4
DILATED_PATTERNS = ((128, 1), (512, 4), (2048, 16))
Q_BLOCK = 64
N_EXPERTS = 256
TOP_K = 8
N_GROUPS = 8
TOPK_GROUP = 4
D_EXPERT = 512
D_SHARED = 512
ROUTED_SCALE = 2.5
EXPERT_ROWS = 128
NORM_EPS = 1e-6

kernel_name = "hybrid_fourier_dilated_moe_adaln_block"


def rmsnorm(x, g):
    x32 = x.astype(jnp.float32)
    y = x32 * lax.rsqrt(jnp.mean(x32 * x32, axis=-1, keepdims=True) + NORM_EPS)
    return (y * g.astype(jnp.float32)).astype(x.dtype)


def partial_rope(t, positions):
    half = ROPE_DIM // 2
    inv_freq = ROPE_THETA ** (-jnp.arange(0, ROPE_DIM, 2, dtype=jnp.float32) / ROPE_DIM)
    ang = positions.astype(jnp.float32)[:, None, :, None] * inv_freq
    cos, sin = jnp.cos(ang), jnp.sin(ang)
    t32 = t.astype(jnp.float32)
    x1, x2, rest = t32[..., :half], t32[..., half:ROPE_DIM], t32[..., ROPE_DIM:]
    out = jnp.concatenate([x1 * cos - x2 * sin, x2 * cos + x1 * sin, rest], axis=-1)
    return out.astype(t.dtype)


def dilated_branch(q, k, v, dil, half):
    b_, h_, s_, hd = q.shape
    L = s_ // dil
    blk = math.gcd(L, Q_BLOCK)
    nblk = L // blk
    span = blk + 2 * half

    def to_res(t):
        return t.reshape(b_, h_, L, dil, hd).transpose(0, 1, 3, 2, 4)

    qr = to_res(q).reshape(b_, h_, dil, nblk, blk, hd)
    pad = ((0, 0), (0, 0), (0, 0), (half, half), (0, 0))
    kp = jnp.pad(to_res(k), pad)
    vp = jnp.pad(to_res(v), pad)
    idx = jnp.arange(nblk)[:, None] * blk + jnp.arange(span)[None, :]
    kb = kp[:, :, :, idx]
    vb = vp[:, :, :, idx]
    s = jnp.einsum('bhrnqd,bhrnkd->bhrnqk', qr, kb,
                   preferred_element_type=jnp.float32) * (hd ** -0.5)
    a = jnp.arange(blk)[:, None]
    j = jnp.arange(span)[None, :]
    rel = j - half - a
    key_pos = jnp.arange(nblk)[:, None, None] * blk + j[None] - half
    valid = (jnp.abs(rel) <= half)[None] & (key_pos >= 0) & (key_pos < L)
    s = jnp.where(valid, s, -jnp.inf)
    m = jnp.max(s, axis=-1, keepdims=True)
    p = jnp.exp(s - m)
    den = jnp.sum(p, axis=-1, keepdims=True)
    o = jnp.einsum('bhrnqk,bhrnkd->bhrnqd', p, vb.astype(jnp.float32)) / den
    lse = (m + jnp.log(den))[..., 0]
    o = o.reshape(b_, h_, dil, L, hd).transpose(0, 1, 3, 2, 4).reshape(b_, h_, s_, hd)
    lse = lse.reshape(b_, h_, dil, L).transpose(0, 1, 3, 2).reshape(b_, h_, s_)
    return o, lse


def token_mixer(h, positions, w_in, g_fourier_out, g_attn_out, w_out):
    b_, s_, _ = h.shape
    proj = h @ w_in
    f_in = proj[..., :FOURIER_W]
    q, k, v = jnp.split(proj[..., FOURIER_W:], 3, axis=-1)

    f = f_in.astype(jnp.float32).reshape(b_, s_, N_FOURIER_GROUPS, FOURIER_GROUP_W)
    f = jnp.fft.fft2(f, axes=(1, 3), norm='ortho').real
    f_out = rmsnorm(f.reshape(b_, s_, FOURIER_W).astype(h.dtype), g_fourier_out)

    def heads(t):
        return t.reshape(b_, s_, N_HEADS, HEAD_DIM).transpose(0, 2, 1, 3)
    qh = partial_rope(heads(q), positions)
    kh = partial_rope(heads(k), positions)
    vh = heads(v)
    outs, lses = [], []
    for window, dil in DILATED_PATTERNS:
        o, l = dilated_branch(qh, kh, vh, dil, window // (2 * dil))
        outs.append(o)
        lses.append(l)
    wts = jax.nn.softmax(jnp.stack(lses, axis=0), axis=0)
    o = jnp.sum(wts[..., None] * jnp.stack(outs, axis=0), axis=0)
    a_out = o.transpose(0, 2, 1, 3).reshape(b_, s_, ATTN_W).astype(h.dtype)
    a_out = rmsnorm(a_out, g_attn_out)

    return jnp.concatenate([f_out, a_out], axis=-1) @ w_out


def moe_ffn(h, w_router, b_router, w_gate, w_up, w_down,
            w_shared_gate, w_shared_up, w_shared_down):
    b_, s_, d_ = h.shape
    T = b_ * s_
    hf = h.reshape(T, d_)

    scores = jax.nn.sigmoid(jnp.dot(hf.astype(jnp.float32), w_router.astype(jnp.float32)))
    biased = scores + b_router.astype(jnp.float32)
    grp = biased.reshape(T, N_GROUPS, N_EXPERTS // N_GROUPS)
    grp_score = jnp.sum(lax.top_k(grp, 2)[0], axis=-1)
    gidx = lax.top_k(grp_score, TOPK_GROUP)[1]
    gmask = jnp.sum(jax.nn.one_hot(gidx, N_GROUPS, dtype=jnp.float32), axis=1) > 0
    emask = jnp.repeat(gmask, N_EXPERTS // N_GROUPS, axis=1)
    sel = lax.top_k(jnp.where(emask, biased, -jnp.inf), TOP_K)[1]
    wsel = jnp.take_along_axis(scores, sel, axis=1)
    wsel = wsel / jnp.sum(wsel, axis=-1, keepdims=True) * ROUTED_SCALE

    TK = T * TOP_K
    e_flat = sel.reshape(TK)
    tok_flat = jnp.repeat(jnp.arange(T, dtype=jnp.int32), TOP_K)
    w_flat = wsel.reshape(TK)
    order = jnp.argsort(e_flat)
    e_s, tok_s, w_s = e_flat[order], tok_flat[order], w_flat[order]
    counts = jnp.bincount(e_flat, length=N_EXPERTS)
    starts = jnp.cumsum(counts) - counts
    padded = ((counts + EXPERT_ROWS - 1) // EXPERT_ROWS) * EXPERT_ROWS
    padded_end = jnp.cumsum(padded)
    padded_start = padded_end - padded
    rank = jnp.arange(TK) - starts[e_s]
    dest = padded_start[e_s] + rank
    R = TK + N_EXPERTS * EXPERT_ROWS
    n_blocks = R // EXPERT_ROWS
    tok_pad = jnp.full((R,), T, dtype=jnp.int32).at[dest].set(tok_s)
    w_pad = jnp.zeros((R,), jnp.float32).at[dest].set(w_s)
    block_expert = jnp.clip(
        jnp.searchsorted(padded_end, jnp.arange(n_blocks) * EXPERT_ROWS, side='right'),
        0, N_EXPERTS - 1)
    x_ext = jnp.concatenate([hf, jnp.zeros((1, d_), hf.dtype)], axis=0)

    def expert_block(args):
        tok, e, wt = args
        xb = x_ext[tok]
        a = xb @ w_gate[e]
        u = xb @ w_up[e]
        y = (jax.nn.silu(a) * u) @ w_down[e]
        return (y.astype(jnp.float32) * wt[:, None])

    ys = lax.map(expert_block, (tok_pad.reshape(n_blocks, EXPERT_ROWS), block_expert,
                                w_pad.reshape(n_blocks, EXPERT_ROWS)))
    routed = jax.ops.segment_sum(ys.reshape(R, d_), tok_pad, num_segments=T + 1)[:T]

    shared = (jax.nn.silu(hf @ w_shared_gate) * (hf @ w_shared_up)) @ w_shared_down
    return (routed + shared.astype(jnp.float32)).astype(h.dtype).reshape(b_, s_, d_)


def setup_inputs(seed: int = 0) -> dict:
    key = jax.random.key(seed)
    ks = jax.random.split(key, 24)
    f32 = jnp.float32
    n = lambda k, shape, s: jax.random.normal(k, shape, f32) * s
    x = n(ks[0], (BATCH, SEQ, D_MODEL), 1.0)
    c = n(ks[1], (BATCH, D_MODEL), 1.0)
    offset = jax.random.randint(ks[2], (BATCH, 1), 0, SEQ, dtype=jnp.int32)
    positions = offset + jnp.arange(SEQ, dtype=jnp.int32)[None, :]
    return {
        "x": x,
        "c": c,
        "positions": positions,
        "w_ada": n(ks[3], (D_MODEL, 6 * D_MODEL), 0.5 * D_MODEL ** -0.5),
        "b_ada": n(ks[4], (6 * D_MODEL,), 0.02),
        "g_mix": 1.0 + n(ks[5], (D_MODEL,), 0.05),
        "g_ffn": 1.0 + n(ks[6], (D_MODEL,), 0.05),
        "g_final": 1.0 + n(ks[7], (D_MODEL,), 0.05),
        "w_in": n(ks[8], (D_MODEL, FOURIER_W + 3 * ATTN_W), D_MODEL ** -0.5),
        "g_fourier_out": 1.0 + n(ks[9], (FOURIER_W,), 0.05),
        "g_attn_out": 1.0 + n(ks[10], (ATTN_W,), 0.05),
        "w_out": n(ks[11], (MIX_W, D_MODEL), MIX_W ** -0.5),
        "w_router": n(ks[12], (D_MODEL, N_EXPERTS), D_MODEL ** -0.5),
        "b_router": n(ks[13], (N_EXPERTS,), 0.01),
        "w_gate": n(ks[14], (N_EXPERTS, D_MODEL, D_EXPERT), D_MODEL ** -0.5),
        "w_up": n(ks[15], (N_EXPERTS, D_MODEL, D_EXPERT), D_MODEL ** -0.5),
        "w_down": n(ks[16], (N_EXPERTS, D_EXPERT, D_MODEL), D_EXPERT ** -0.5),
        "w_shared_gate": n(ks[17], (D_MODEL, D_SHARED), D_MODEL ** -0.5),
        "w_shared_up": n(ks[18], (D_MODEL, D_SHARED), D_MODEL ** -0.5),
        "w_shared_down": n(ks[19], (D_SHARED, D_MODEL), D_SHARED ** -0.5),
    }


def reference(x, c, positions, w_ada, b_ada, g_mix, g_ffn, g_final, w_in,
              g_fourier_out, g_attn_out, w_out, w_router, b_router, w_gate, w_up,
              w_down, w_shared_gate, w_shared_up, w_shared_down):
    mod = jax.nn.silu(c) @ w_ada + b_ada
    shift1, scale1, gate1, shift2, scale2, gate2 = jnp.split(mod[:, None, :], 6, axis=-1)
    for _ in range(DEPTH):
        h = rmsnorm(x, g_mix) * (1.0 + scale1) + shift1
        x = x + gate1 * token_mixer(h, positions, w_in, g_fourier_out, g_attn_out, w_out)
        h = rmsnorm(x, g_ffn) * (1.0 + scale2) + shift2
        x = x + gate2 * moe_ffn(h, w_router, b_router, w_gate, w_up, w_down,
                                w_shared_gate, w_shared_up, w_shared_down)
    return rmsnorm(x, g_final)
```

```python
import functools
import math

import jax
import jax.numpy as jnp
from jax import lax
from jax.experimental import pallas as pl
from jax.experimental.pallas import tpu as pltpu

F32, BF16, I32 = jnp.float32, jnp.bfloat16, jnp.int32

FOURIER_W = 1024
FOURIER_GROUP_W = 256
N_FOURIER_GROUPS = 4
ATTN_W = 1024
HEAD_DIM = 128
N_HEADS = 8
ROPE_THETA = 500000.0
ROPE_DIM = 32
DILATED_PATTERNS = ((128, 1), (512, 4), (2048, 16))
N_EXPERTS = 256
TOP_K = 8
N_GROUPS = 8
TOPK_GROUP = 4
GROUP_SIZE = N_EXPERTS // N_GROUPS
D_EXPERT = 512
ROUTED_SCALE = 2.5
NORM_EPS = 1e-6

LANES = 128
VMEM_BUDGET = 56 << 20

EXPERT_ROWS = 256
ATTN_QB = 128
ATTN_HALF = 64
ATTN_SPAN = ATTN_QB + 2 * ATTN_HALF
ATTN_PAD = ATTN_HALF * max(d for _, d in DILATED_PATTERNS)

_SDS = jax.ShapeDtypeStruct


def _params(n_axes, vmem=VMEM_BUDGET):
    return pltpu.CompilerParams(dimension_semantics=("arbitrary",) * n_axes, vmem_limit_bytes=vmem)


def _rms(x, g):
    return x * lax.rsqrt(jnp.mean(x * x, axis=-1, keepdims=True) + NORM_EPS) * g


def _adaln_body(c_ref, w_ref, b_ref, o_ref):
    c = c_ref[...]
    o_ref[...] = jnp.dot(c * jax.nn.sigmoid(c), w_ref[...], preferred_element_type=F32) + b_ref[...]


def _adaln(c, w_ada, b_ada):
    b, d = c.shape
    n = w_ada.shape[1]
    tn = 1024
    cp = jnp.pad(c, ((0, 8 - b), (0, 0)))
    out = pl.pallas_call(
        _adaln_body,
        out_shape=_SDS((8, n), F32),
        grid=(n // tn,),
        in_specs=[pl.BlockSpec((8, d), lambda j: (0, 0)),
                  pl.BlockSpec((d, tn), lambda j: (0, j)),
                  pl.BlockSpec((1, tn), lambda j: (0, j))],
        out_specs=pl.BlockSpec((8, tn), lambda j: (0, j)),
        compiler_params=_params(1),
        name="adaln",
    )(cp, w_ada, b_ada.reshape(1, n))
    return out[:b]


def _inproj_body(x_ref, mod_ref, g_ref, pos_ref, invf_ref, sgn_ref, w_ref, o_ref,
                 h_sc, cos_sc, sin_sc, *, q_scale):
    j = pl.program_id(1)
    tm = x_ref.shape[0]

    @pl.when(j == 0)
    def _():
        h = _rms(x_ref[...], g_ref[...]) * (1.0 + mod_ref[1:2, :]) + mod_ref[0:1, :]
        h_sc[...] = h.astype(BF16)
        ang = pos_ref[...].astype(F32) * invf_ref[...]
        cos_sc[...] = jnp.cos(ang)
        sin_sc[...] = jnp.sin(ang) * sgn_ref[...]

    acc = jnp.dot(h_sc[...], w_ref[...], preferred_element_type=F32)
    is_rope = (j == 1) | (j == 2)

    @pl.when(jnp.logical_not(is_rope))
    def _():
        o_ref[...] = acc

    @pl.when(is_rope)
    def _():
        scale = jnp.where(j == 1, q_scale, 1.0).astype(F32)
        cs = cos_sc[...] * scale
        sn = sin_sc[...] * scale
        lane = lax.broadcasted_iota(I32, (tm, LANES), 1)
        half = ROPE_DIM // 2
        for hb in range(N_HEADS):
            t = acc[:, hb * HEAD_DIM:(hb + 1) * HEAD_DIM]
            rot = jnp.where(lane < half, pltpu.roll(t, LANES - half, 1), pltpu.roll(t, half, 1))
            o_ref[:, hb * HEAD_DIM:(hb + 1) * HEAD_DIM] = t * cs + rot * sn


def _inproj(x2, mod3, g_mix, pos2, w_in_bf, seq):
    t, d = x2.shape
    n = w_in_bf.shape[1]
    tm, tn = 512, 1024
    assert tn == FOURIER_W == ATTN_W and HEAD_DIM == LANES
    half = ROPE_DIM // 2
    inv_freq = ROPE_THETA ** (-jnp.arange(0, ROPE_DIM, 2, dtype=F32) / ROPE_DIM)
    invf = jnp.concatenate([inv_freq, inv_freq, jnp.zeros((LANES - ROPE_DIM,), F32)]).reshape(1, LANES)
    sgn = jnp.concatenate([-jnp.ones((half,), F32), jnp.ones((half,), F32),
                           jnp.zeros((LANES - ROPE_DIM,), F32)]).reshape(1, LANES)
    tpb = seq // tm
    return pl.pallas_call(
        functools.partial(_inproj_body, q_scale=HEAD_DIM ** -0.5),
        out_shape=_SDS((t, n), F32),
        grid=(t // tm, n // tn),
        in_specs=[pl.BlockSpec((tm, d), lambda i, j: (i, 0)),
                  pl.BlockSpec((None, 6, d), lambda i, j: (i // tpb, 0, 0)),
                  pl.BlockSpec((1, d), lambda i, j: (0, 0)),
                  pl.BlockSpec((tm, 1), lambda i, j: (i, 0)),
                  pl.BlockSpec((1, LANES), lambda i, j: (0, 0)),
                  pl.BlockSpec((1, LANES), lambda i, j: (0, 0)),
                  pl.BlockSpec((d, tn), lambda i, j: (0, j))],
        out_specs=pl.BlockSpec((tm, tn), lambda i, j: (i, j)),
        scratch_shapes=[pltpu.VMEM((tm, d), BF16), pltpu.VMEM((tm, LANES), F32), pltpu.VMEM((tm, LANES), F32)],
        compiler_params=_params(2),
        name="inproj",
    )(x2, mod3, g_mix.reshape(1, d), pos2, invf, sgn, w_in_bf)


def _dft_tables(n):
    idx = jnp.arange(n, dtype=I32)
    ang = ((idx[:, None] * idx[None, :]) % n).astype(F32) * (2.0 * math.pi / n)
    return jnp.cos(ang), jnp.sin(ang)


def _chdft_body(p_ref, cs_ref, xc_ref, xs_ref):
    gw = FOURIER_GROUP_W
    for g in range(N_FOURIER_GROUPS):
        f = p_ref[:, g * gw:(g + 1) * gw].astype(BF16)
        z = jnp.dot(f, cs_ref[...], preferred_element_type=F32)
        xc_ref[:, g * gw:(g + 1) * gw] = z[:, :gw].astype(BF16)
        xs_ref[:, g * gw:(g + 1) * gw] = z[:, gw:].astype(BF16)


def _chdft(proj, cs_ch):
    t = proj.shape[0]
    tm = 1024
    return pl.pallas_call(
        _chdft_body,
        out_shape=(_SDS((t, FOURIER_W), BF16), _SDS((t, FOURIER_W), BF16)),
        grid=(t // tm,),
        in_specs=[pl.BlockSpec((tm, FOURIER_W), lambda i: (i, 0)),
                  pl.BlockSpec(cs_ch.shape, lambda i: (0, 0))],
        out_specs=(pl.BlockSpec((tm, FOURIER_W), lambda i: (i, 0)),
                   pl.BlockSpec((tm, FOURIER_W), lambda i: (i, 0))),
        compiler_params=_params(1),
        name="chdft",
    )(proj, cs_ch)


def _seqdft_body(c_ref, ns_ref, xc_ref, xs_ref, g_ref, o_ref, acc, *, scale):
    k = pl.program_id(2)

    @pl.when(k == 0)
    def _():
        acc[...] = jnp.zeros_like(acc)

    acc[...] += (jnp.dot(c_ref[...], xc_ref[...], preferred_element_type=F32)
                 + jnp.dot(ns_ref[...], xs_ref[...], preferred_element_type=F32))

    @pl.when(k == pl.num_programs(2) - 1)
    def _():
        o_ref[...] = _rms(acc[...] * scale, g_ref[...]).astype(BF16)


def _seqdft(xc3, xs3, c_seq, ns_seq, g_fourier):
    b, s, w = xc3.shape
    tm = tk = 1024
    return pl.pallas_call(
        functools.partial(_seqdft_body, scale=1.0 / math.sqrt(s * FOURIER_GROUP_W)),
        out_shape=_SDS((b, s, w), BF16),
        grid=(b, s // tm, s // tk),
        in_specs=[pl.BlockSpec((tm, tk), lambda bi, i, k: (i, k)),
                  pl.BlockSpec((tm, tk), lambda bi, i, k: (i, k)),
                  pl.BlockSpec((None, tk, w), lambda bi, i, k: (bi, k, 0)),
                  pl.BlockSpec((None, tk, w), lambda bi, i, k: (bi, k, 0)),
                  pl.BlockSpec((1, w), lambda bi, i, k: (0, 0))],
        out_specs=pl.BlockSpec((None, tm, w), lambda bi, i, k: (bi, i, 0)),
        scratch_shapes=[pltpu.VMEM((tm, w), F32)],
        compiler_params=_params(3),
        name="seqdft",
    )(c_seq, ns_seq, xc3, xs3, g_fourier.reshape(1, w))


def _attn_body(q_ref, k_ref, v_ref, o_ref, kpad, vpad, ob0, ob1, ob2, lb0, lb1, lb2):
    s = q_ref.shape[0]
    obufs, lbufs = (ob0, ob1, ob2), (lb0, lb1, lb2)
    qb, span, half, pad = ATTN_QB, ATTN_SPAN, ATTN_HALF, ATTN_PAD
    chunk = 512

    zeros = jnp.zeros((pad, HEAD_DIM), F32)
    for buf in (kpad, vpad):
        buf[0:pad, :] = zeros
        buf[pad + s:pad + s + pad, :] = zeros

    def copy_in(c, carry):
        r0 = pl.multiple_of(c * chunk, chunk)
        kpad[pl.ds(pad + r0, chunk), :] = k_ref[pl.ds(r0, chunk), :]
        vpad[pl.ds(pad + r0, chunk), :] = v_ref[pl.ds(r0, chunk), :]
        return carry

    lax.fori_loop(0, s // chunk, copy_in, 0)

    row = lax.broadcasted_iota(I32, (qb, span), 0)
    col = lax.broadcasted_iota(I32, (qb, span), 1)
    band = (col >= row) & (col <= row + 2 * half)

    for p, (window, dil) in enumerate(DILATED_PATTERNS):
        assert window // (2 * dil) == half
        length = s // dil
        nblk = length // qb
        shift = nblk.bit_length() - 1
        assert nblk == 1 << shift
        obuf, lbuf = obufs[p], lbufs[p]

        def rows(start, size):
            return pl.ds(start, size) if dil == 1 else pl.ds(start, size, stride=dil)

        def block(i, carry):
            r = i >> shift
            n = i & (nblk - 1)
            q0 = r + n * (qb * dil)
            k0 = pad + r + (n * qb - half) * dil
            q = q_ref[rows(q0, qb), :].astype(BF16)
            ks = kpad[rows(k0, span), :].astype(BF16)
            vs = vpad[rows(k0, span), :].astype(BF16)
            sc = lax.dot_general(q, ks, (((1,), (1,)), ((), ())), preferred_element_type=F32)
            lo = half - n * qb
            valid = band & (col >= lo) & (col < lo + length)
            sc = jnp.where(valid, sc, -jnp.inf)
            m = jnp.max(sc, axis=-1, keepdims=True)
            e = jnp.exp(sc - m)
            den = jnp.sum(e, axis=-1, keepdims=True)
            o = jnp.dot(e.astype(BF16), vs, preferred_element_type=F32) / den
            obuf[rows(q0, qb), :] = o
            lbuf[rows(q0, qb), :] = jnp.broadcast_to(m + jnp.log(den), (qb, HEAD_DIM))
            return carry

        lax.fori_loop(0, dil * nblk, block, 0, unroll=2)

    def mix(c, carry):
        r0 = pl.multiple_of(c * chunk, chunk)
        sl = pl.ds(r0, chunk)
        l0, l1, l2 = lb0[sl, :], lb1[sl, :], lb2[sl, :]
        m = jnp.maximum(jnp.maximum(l0, l1), l2)
        e0, e1, e2 = jnp.exp(l0 - m), jnp.exp(l1 - m), jnp.exp(l2 - m)
        o_ref[sl, :] = (e0 * ob0[sl, :] + e1 * ob1[sl, :] + e2 * ob2[sl, :]) / (e0 + e1 + e2)
        return carry

    lax.fori_loop(0, s // chunk, mix, 0)


def _attention(proj3):
    b, s, _ = proj3.shape
    qoff, koff, voff = (FOURIER_W // HEAD_DIM, (FOURIER_W + ATTN_W) // HEAD_DIM,
                        (FOURIER_W + 2 * ATTN_W) // HEAD_DIM)
    head = lambda off: pl.BlockSpec((None, s, HEAD_DIM), lambda bi, h: (bi, 0, off + h))
    full = pltpu.VMEM((s, HEAD_DIM), F32)
    padded = pltpu.VMEM((s + 2 * ATTN_PAD, HEAD_DIM), F32)
    return pl.pallas_call(
        _attn_body,
        out_shape=_SDS((b, s, ATTN_W), F32),
        grid=(b, N_HEADS),
        in_specs=[head(qoff), head(koff), head(voff)],
        out_specs=pl.BlockSpec((None, s, HEAD_DIM), lambda bi, h: (bi, 0, h)),
        scratch_shapes=[padded, padded, full, full, full, full, full, full],
        compiler_params=_params(2),
        name="attn",
    )(proj3, proj3, proj3)


def _outproj_body(x_ref, f_ref, a_ref, mod_ref, ga_ref, gf_ref, wo_ref, wrh_ref, wrl_ref,
                  x1_ref, h2_ref, lg_ref):
    an = _rms(a_ref[...], ga_ref[...]).astype(BF16)
    y = (jnp.dot(f_ref[...], wo_ref[0:FOURIER_W, :], preferred_element_type=F32)
         + jnp.dot(an, wo_ref[FOURIER_W:FOURIER_W + ATTN_W, :], preferred_element_type=F32))
    x1 = x_ref[...] + mod_ref[2:3, :] * y
    x1_ref[...] = x1
    h2 = _rms(x1, gf_ref[...]) * (1.0 + mod_ref[4:5, :]) + mod_ref[3:4, :]
    h2_ref[...] = h2
    hi = h2.astype(BF16)
    lo = (h2 - hi.astype(F32)).astype(BF16)
    lg_ref[...] = (jnp.dot(hi, wrh_ref[...], preferred_element_type=F32)
                   + jnp.dot(lo, wrh_ref[...], preferred_element_type=F32)
                   + jnp.dot(hi, wrl_ref[...], preferred_element_type=F32))


def _outproj(x2, f_out, a_raw, mod3, g_attn, g_ffn, w_out_bf, wr_hi, wr_lo, seq):
    t, d = x2.shape
    tm = 256
    tpb = seq // tm
    row = lambda w: pl.BlockSpec((tm, w), lambda i: (i, 0))
    const = lambda shape: pl.BlockSpec(shape, lambda i: (0,) * len(shape))
    return pl.pallas_call(
        _outproj_body,
        out_shape=(_SDS((t, d), F32), _SDS((t, d), F32), _SDS((t, N_EXPERTS), F32)),
        grid=(t // tm,),
        in_specs=[row(d), row(FOURIER_W), row(ATTN_W),
                  pl.BlockSpec((None, 6, d), lambda i: (i // tpb, 0, 0)),
                  const((1, ATTN_W)), const((1, d)), const(w_out_bf.shape),
                  const(wr_hi.shape), const(wr_lo.shape)],
        out_specs=(row(d), row(d), row(N_EXPERTS)),
        compiler_params=_params(1),
        name="outproj",
    )(x2, f_out, a_raw, mod3, g_attn.reshape(1, ATTN_W), g_ffn.reshape(1, d), w_out_bf, wr_hi, wr_lo)


def _route_body(lg_ref, b_ref, sel_ref, w_ref, rank_ref, cnt_ref, carry):
    i = pl.program_id(0)
    tm = lg_ref.shape[0]
    ne = N_EXPERTS

    @pl.when(i == 0)
    def _():
        carry[...] = jnp.zeros_like(carry)

    scores = jax.nn.sigmoid(lg_ref[...].T)
    biased = scores + b_ref[...]
    groups = [biased[g * GROUP_SIZE:(g + 1) * GROUP_SIZE, :] for g in range(N_GROUPS)]
    gscore = []
    for blk in groups:
        m1 = jnp.max(blk, axis=0, keepdims=True)
        is1 = blk == m1
        n1 = jnp.sum(is1.astype(F32), axis=0, keepdims=True)
        m2 = jnp.max(jnp.where(is1, -jnp.inf, blk), axis=0, keepdims=True)
        gscore.append(m1 + jnp.where(n1 >= 2.0, m1, m2))
    parts = []
    for g in range(N_GROUPS):
        beaten = jnp.zeros((1, tm), F32)
        for gp in range(N_GROUPS):
            if gp != g:
                wins = (gscore[gp] >= gscore[g]) if gp < g else (gscore[gp] > gscore[g])
                beaten += wins.astype(F32)
        parts.append(jnp.where(beaten < float(TOPK_GROUP), groups[g], -jnp.inf))
    masked = jnp.concatenate(parts, axis=0)

    eidx = lax.broadcasted_iota(I32, (ne, tm), 0).astype(F32)
    chosen = jnp.zeros((ne, tm), F32)
    sels, ws = [], []
    for _ in range(TOP_K):
        m = jnp.max(masked, axis=0, keepdims=True)
        idx = jnp.min(jnp.where(masked == m, eidx, float(ne)), axis=0, keepdims=True)
        hit = eidx == idx
        sels.append(idx)
        ws.append(jnp.sum(jnp.where(hit, scores, 0.0), axis=0, keepdims=True))
        masked = jnp.where(hit, -jnp.inf, masked)
        chosen = jnp.where(hit, 1.0, chosen)
    wsum = functools.reduce(lambda a, b: a + b, ws)

    upper = (lax.broadcasted_iota(I32, (tm, tm), 0) <= lax.broadcasted_iota(I32, (tm, tm), 1)).astype(BF16)
    incl = jnp.dot(chosen.astype(BF16), upper, preferred_element_type=F32)
    rank_all = carry[...] + incl - chosen
    for k in range(TOP_K):
        sel_ref[k:k + 1, :] = sels[k].astype(I32)
        w_ref[k:k + 1, :] = ws[k] / wsum * ROUTED_SCALE
        rk = jnp.sum(jnp.where(eidx == sels[k], rank_all, 0.0), axis=0, keepdims=True)
        rank_ref[k:k + 1, :] = rk.astype(I32)
    carry[...] = carry[...] + jnp.sum(chosen, axis=1, keepdims=True)
    cnt_ref[...] = carry[...]


def _route(logits, b_router):
    t, ne = logits.shape
    tm = 512
    tile = lambda dt: pl.BlockSpec((TOP_K, tm), lambda i: (0, i))
    return pl.pallas_call(
        _route_body,
        out_shape=(_SDS((TOP_K, t), I32), _SDS((TOP_K, t), F32), _SDS((TOP_K, t), I32), _SDS((ne, 1), F32)),
        grid=(t // tm,),
        in_specs=[pl.BlockSpec((tm, ne), lambda i: (i, 0)), pl.BlockSpec((ne, 1), lambda i: (0, 0))],
        out_specs=(tile(I32), tile(F32), tile(I32), pl.BlockSpec((ne, 1), lambda i: (0, 0))),
        scratch_shapes=[pltpu.VMEM((ne, 1), F32)],
        compiler_params=_params(1),
        name="route",
    )(logits, b_router.reshape(ne, 1))


def _dest_body(cnt_ref, sel_ref, rank_ref, dest_ref, be_ref, first_ref, nblk_ref, nused_ref, *, n_blocks):
    ne = N_EXPERTS
    tm = sel_ref.shape[1]
    g = EXPERT_ROWS
    cnt = cnt_ref[...].astype(I32)
    assert g & (g - 1) == 0
    nb = ((cnt + (g - 1)) >> (g.bit_length() - 1)).astype(F32)
    lower = (lax.broadcasted_iota(I32, (ne, ne), 1) < lax.broadcasted_iota(I32, (ne, ne), 0)).astype(BF16)
    first = jnp.dot(lower, jnp.broadcast_to(nb, (ne, LANES)).astype(BF16),
                    preferred_element_type=F32)[:, 0:1]
    row0 = first * float(g)
    eidx = lax.broadcasted_iota(I32, (ne, tm), 0)
    for k in range(TOP_K):
        base = jnp.sum(jnp.where(eidx == sel_ref[k:k + 1, :], row0, 0.0), axis=0, keepdims=True)
        dest_ref[k:k + 1, :] = base.astype(I32) + rank_ref[k:k + 1, :]
    blk = lax.broadcasted_iota(I32, (ne, n_blocks), 1).astype(F32)
    owner = lax.broadcasted_iota(I32, (ne, n_blocks), 0).astype(F32)
    inside = (blk >= first) & (blk < first + nb)
    be_ref[...] = jnp.sum(jnp.where(inside, owner, 0.0), axis=0, keepdims=True).astype(I32)
    first_ref[...] = first.astype(I32)
    nblk_ref[...] = nb.astype(I32)
    nused_ref[...] = jnp.broadcast_to(jnp.sum(nb, axis=0, keepdims=True), (1, LANES)).astype(I32)


def _dest(counts, sel, rank, n_blocks):
    t = sel.shape[1]
    ne = N_EXPERTS
    tm = 2048
    tile = pl.BlockSpec((TOP_K, tm), lambda i: (0, i))
    const = lambda shape: pl.BlockSpec(shape, lambda i: (0, 0))
    return pl.pallas_call(
        functools.partial(_dest_body, n_blocks=n_blocks),
        out_shape=(_SDS((TOP_K, t), I32), _SDS((1, n_blocks), I32), _SDS((ne, 1), I32),
                   _SDS((ne, 1), I32), _SDS((1, LANES), I32)),
        grid=(t // tm,),
        in_specs=[const((ne, 1)), tile, tile],
        out_specs=(tile, const((1, n_blocks)), const((ne, 1)), const((ne, 1)), const((1, LANES))),
        compiler_params=_params(1),
        name="dest",
    )(counts, sel, rank)


def _dispatch_body(first_sm, nblk_sm, cnt_sm, dest_hbm, h_ref, xs_hbm, dsm, zrow, sem_i, sem_r, sem_z):
    i = pl.program_id(0)
    tm = h_ref.shape[0]
    g = EXPERT_ROWS
    idx_copy = pltpu.make_async_copy(dest_hbm.at[i], dsm, sem_i)
    idx_copy.start()

    @pl.when(i == 0)
    def _():
        zrow[...] = jnp.zeros_like(zrow)

        def per_expert(wait):
            def body(e, carry):
                base = first_sm[e] * g

                def one(r, c):
                    cp = pltpu.make_async_copy(zrow, xs_hbm.at[pl.ds(base + r, 1)], sem_z)
                    cp.wait() if wait else cp.start()
                    return c

                return lax.fori_loop(cnt_sm[e], nblk_sm[e] * g, one, carry)
            lax.fori_loop(0, N_EXPERTS, body, 0)

        per_expert(False)
        per_expert(True)

    idx_copy.wait()

    def scatter(tk, carry):
        for k in range(TOP_K):
            pltpu.make_async_copy(h_ref.at[pl.ds(tk, 1)], xs_hbm.at[pl.ds(dsm[k * tm + tk], 1)], sem_r).start()
        return carry

    lax.fori_loop(0, tm, scatter, 0)
    for k in range(TOP_K):
        pltpu.make_async_copy(h_ref, xs_hbm.at[pl.ds(0, tm)], sem_r).wait()


def _dispatch(first, nblk, cnt, dest_tiles, h2, n_rows):
    t, d = h2.shape
    n_tiles, per_tile = dest_tiles.shape
    tm = per_tile // TOP_K
    return pl.pallas_call(
        _dispatch_body,
        out_shape=_SDS((n_rows, d), F32),
        grid_spec=pltpu.PrefetchScalarGridSpec(
            num_scalar_prefetch=3,
            grid=(n_tiles,),
            in_specs=[pl.BlockSpec(memory_space=pl.ANY),
                      pl.BlockSpec((tm, d), lambda i, *_: (i, 0))],
            out_specs=pl.BlockSpec(memory_space=pl.ANY),
            scratch_shapes=[pltpu.SMEM((per_tile,), I32), pltpu.VMEM((1, d), F32),
                            pltpu.SemaphoreType.DMA, pltpu.SemaphoreType.DMA, pltpu.SemaphoreType.DMA]),
        compiler_params=_params(1),
        name="dispatch",
    )(first, nblk, cnt, dest_tiles, h2)


def _experts_body(be_sm, nused_sm, x_ref, wg_ref, wu_ref, wd_ref, y_ref, wg_bf, wu_bf, wd_bf):
    i = pl.program_id(0)
    used = i < nused_sm[0]
    prev = be_sm[jnp.maximum(i - 1, 0)]
    fresh = used & ((i == 0) | (be_sm[i] != prev))

    @pl.when(fresh)
    def _():
        wg_bf[...] = wg_ref[...].astype(BF16)
        wu_bf[...] = wu_ref[...].astype(BF16)
        wd_bf[...] = wd_ref[...].astype(BF16)

    @pl.when(used)
    def _():
        x = x_ref[...].astype(BF16)
        a = jnp.dot(x, wg_bf[...], preferred_element_type=F32)
        u = jnp.dot(x, wu_bf[...], preferred_element_type=F32)
        hmid = (a * jax.nn.sigmoid(a) * u).astype(BF16)
        y_ref[...] = jnp.dot(hmid, wd_bf[...], preferred_element_type=F32)


def _experts(be, nused, xs, w_gate, w_up, w_down):
    n_rows, d = xs.shape
    g = EXPERT_ROWS
    n_blocks = n_rows // g
    blk = lambda i, be, nu: jnp.minimum(i, nu[0] - 1)
    wspec = lambda shape: pl.BlockSpec((None,) + shape, lambda i, be, nu: (be[blk(i, be, nu)], 0, 0))
    return pl.pallas_call(
        _experts_body,
        out_shape=_SDS((n_rows, d), F32),
        grid_spec=pltpu.PrefetchScalarGridSpec(
            num_scalar_prefetch=2,
            grid=(n_blocks,),
            in_specs=[pl.BlockSpec((g, d), lambda i, be, nu: (blk(i, be, nu), 0)),
                      wspec((d, D_EXPERT)), wspec((d, D_EXPERT)), wspec((D_EXPERT, d))],
            out_specs=pl.BlockSpec((g, d), lambda i, be, nu: (blk(i, be, nu), 0)),
            scratch_shapes=[pltpu.VMEM((d, D_EXPERT), BF16), pltpu.VMEM((d, D_EXPERT), BF16),
                            pltpu.VMEM((D_EXPERT, d), BF16)]),
        compiler_params=_params(1),
        name="experts",
    )(be, nused, xs, w_gate, w_up, w_down)


def _combine_body(dest_hbm, ys_hbm, x1_ref, h_ref, w_ref, mod_ref, gfin_ref, sg_ref, su_ref, sd_ref, o_ref,
                  dsm, gbuf, sem_i, sem_r):
    i = pl.program_id(0)
    tm = x1_ref.shape[0]
    idx_copy = pltpu.make_async_copy(dest_hbm.at[i], dsm, sem_i)
    idx_copy.start()
    idx_copy.wait()

    def gather(tk, carry):
        for k in range(TOP_K):
            pltpu.make_async_copy(ys_hbm.at[pl.ds(dsm[k * tm + tk], 1)], gbuf.at[k, pl.ds(tk, 1)], sem_r).start()
        return carry

    lax.fori_loop(0, tm, gather, 0)

    h = h_ref[...].astype(BF16)
    a = jnp.dot(h, sg_ref[...], preferred_element_type=F32)
    u = jnp.dot(h, su_ref[...], preferred_element_type=F32)
    total = jnp.dot((a * jax.nn.sigmoid(a) * u).astype(BF16), sd_ref[...], preferred_element_type=F32)

    for k in range(TOP_K):
        pltpu.make_async_copy(ys_hbm.at[pl.ds(0, tm)], gbuf.at[k], sem_r).wait()
    for k in range(TOP_K):
        total = total + gbuf[k] * w_ref[:, k:k + 1]
    o_ref[...] = _rms(x1_ref[...] + mod_ref[5:6, :] * total, gfin_ref[...])


def _combine(dest_tiles, ys, x1, h2, w_tok, mod3, g_final, ws_gate, ws_up, ws_down, seq):
    t, d = x1.shape
    n_tiles, per_tile = dest_tiles.shape
    tm = per_tile // TOP_K
    tpb = seq // tm
    row = lambda w: pl.BlockSpec((tm, w), lambda i: (i, 0))
    const = lambda shape: pl.BlockSpec(shape, lambda i: (0,) * len(shape))
    return pl.pallas_call(
        _combine_body,
        out_shape=_SDS((t, d), F32),
        grid=(n_tiles,),
        in_specs=[pl.BlockSpec(memory_space=pl.ANY), pl.BlockSpec(memory_space=pl.ANY),
                  row(d), row(d), row(TOP_K),
                  pl.BlockSpec((None, 6, d), lambda i: (i // tpb, 0, 0)),
                  const((1, d)), const(ws_gate.shape), const(ws_up.shape), const(ws_down.shape)],
        out_specs=row(d),
        scratch_shapes=[pltpu.SMEM((per_tile,), I32), pltpu.VMEM((TOP_K, tm, d), F32),
                        pltpu.SemaphoreType.DMA, pltpu.SemaphoreType.DMA],
        compiler_params=_params(1),
        name="combine",
    )(dest_tiles, ys, x1, h2, w_tok, mod3, g_final.reshape(1, d), ws_gate, ws_up, ws_down)


def _tile_major(a, tm):
    k, t = a.shape
    return a.reshape(k, t // tm, tm).transpose(1, 0, 2).reshape(t // tm, k * tm)


def kernel(x, c, positions, w_ada, b_ada, g_mix, g_ffn, g_final, w_in, g_fourier_out, g_attn_out, w_out,
           w_router, b_router, w_gate, w_up, w_down, w_shared_gate, w_shared_up, w_shared_down):
    b, s, d = x.shape
    t = b * s
    x2 = x.reshape(t, d)
    mod3 = _adaln(c, w_ada, b_ada).reshape(b, 6, d)

    proj = _inproj(x2, mod3, g_mix, positions.reshape(t, 1), w_in.astype(BF16), s)

    cc, sc = _dft_tables(FOURIER_GROUP_W)
    xc, xs = _chdft(proj, jnp.concatenate([cc, sc], axis=1).astype(BF16))
    cseq, sseq = _dft_tables(s)
    f_out = _seqdft(xc.reshape(b, s, FOURIER_W), xs.reshape(b, s, FOURIER_W),
                    cseq.astype(BF16), (-sseq).astype(BF16), g_fourier_out)

    a_raw = _attention(proj.reshape(b, s, proj.shape[1]))

    wr_hi = w_router.astype(BF16)
    wr_lo = (w_router - wr_hi.astype(F32)).astype(BF16)
    x1, h2, logits = _outproj(x2, f_out.reshape(t, FOURIER_W), a_raw.reshape(t, ATTN_W), mod3,
                              g_attn_out, g_ffn, w_out.astype(BF16), wr_hi, wr_lo, s)

    sel, w_sel, rank, counts = _route(logits, b_router)
    n_blocks = (t * TOP_K) // EXPERT_ROWS + N_EXPERTS
    dest, be, first, nblk, nused = _dest(counts, sel, rank, n_blocks)

    tm_rows = 256
    dest_tiles = _tile_major(dest, tm_rows)
    xs_sorted = _dispatch(first.reshape(-1), nblk.reshape(-1), counts.astype(I32).reshape(-1),
                          dest_tiles, h2, n_blocks * EXPERT_ROWS)
    ys = _experts(be.reshape(-1), nused[0, :1], xs_sorted, w_gate, w_up, w_down)
    out = _combine(dest_tiles, ys, x1, h2, w_sel.T, mod3, g_final,
                   w_shared_gate.astype(BF16), w_shared_up.astype(BF16), w_shared_down.astype(BF16), s)
    return out.reshape(b, s, d)
```

```python
import functools
import math

import jax
import jax.numpy as jnp
from jax import lax
from jax.experimental import pallas as pl
from jax.experimental.pallas import tpu as pltpu

F32, BF16, I32, U32 = jnp.float32, jnp.bfloat16, jnp.int32, jnp.uint32

FOURIER_W = 1024
FOURIER_GROUP_W = 256
N_FOURIER_GROUPS = 4
ATTN_W = 1024
HEAD_DIM = 128
N_HEADS = 8
ROPE_THETA = 500000.0
ROPE_DIM = 32
DILATED_PATTERNS = ((128, 1), (512, 4), (2048, 16))
N_EXPERTS = 256
TOP_K = 8
N_GROUPS = 8
TOPK_GROUP = 4
GROUP_SIZE = N_EXPERTS // N_GROUPS
D_EXPERT = 512
ROUTED_SCALE = 2.5
NORM_EPS = 1e-6

LANES = 128
VMEM_BUDGET = 56 << 20

EXPERT_ROWS = 256
ATTN_QB = 128
ATTN_HALF = 64
ATTN_SPAN = ATTN_QB + 2 * ATTN_HALF
ATTN_PAD = ATTN_HALF * max(d for _, d in DILATED_PATTERNS)

_SDS = jax.ShapeDtypeStruct


def _params(n_axes, vmem=VMEM_BUDGET):
    return pltpu.CompilerParams(dimension_semantics=("arbitrary",) * n_axes, vmem_limit_bytes=vmem)


def _rms(x, g):
    return x * lax.rsqrt(jnp.mean(x * x, axis=-1, keepdims=True) + NORM_EPS) * g


ROW_W = 1024
ROW_SUB = ROW_W // LANES
HIGH_HALF = 0xFFFF0000


def _store_row_tiles(ref, val):
    rows = val.shape[0]
    lo = lax.bitcast_convert_type(val[:, :ROW_W].astype(BF16).astype(F32), U32) >> 16
    hi = lax.bitcast_convert_type(val[:, ROW_W:].astype(BF16).astype(F32), U32) & jnp.uint32(HIGH_HALF)
    word = lo | hi
    for s in range(ROW_SUB):
        ref[pl.ds(s, rows, stride=ROW_SUB), :] = word[:, s * LANES:(s + 1) * LANES]


def _load_row_tile_chunk(ref, rows, s):
    word = ref[pl.ds(s, rows, stride=ROW_SUB), :]
    lo = lax.bitcast_convert_type(word << 16, F32)
    hi = lax.bitcast_convert_type(word & jnp.uint32(HIGH_HALF), F32)
    return lo, hi


def _load_row_tiles_bf16(ref, rows):
    chunks = [_load_row_tile_chunk(ref, rows, s) for s in range(ROW_SUB)]
    lo = jnp.concatenate([c[0].astype(BF16) for c in chunks], axis=1)
    hi = jnp.concatenate([c[1].astype(BF16) for c in chunks], axis=1)
    return lo, hi


def _adaln_body(c_ref, w_ref, b_ref, o_ref):
    c = c_ref[...]
    o_ref[...] = jnp.dot(c * jax.nn.sigmoid(c), w_ref[...], preferred_element_type=F32) + b_ref[...]


def _adaln(c, w_ada, b_ada):
    b, d = c.shape
    n = w_ada.shape[1]
    tn = 1024
    cp = jnp.pad(c, ((0, 8 - b), (0, 0)))
    out = pl.pallas_call(
        _adaln_body,
        out_shape=_SDS((8, n), F32),
        grid=(n // tn,),
        in_specs=[pl.BlockSpec((8, d), lambda j: (0, 0)),
                  pl.BlockSpec((d, tn), lambda j: (0, j)),
                  pl.BlockSpec((1, tn), lambda j: (0, j))],
        out_specs=pl.BlockSpec((8, tn), lambda j: (0, j)),
        compiler_params=_params(1),
        name="adaln",
    )(cp, w_ada, b_ada.reshape(1, n))
    return out[:b]


def _inproj_body(x_ref, mod_ref, g_ref, pos_ref, invf_ref, sgn_ref, w_ref, o_ref,
                 h_sc, cos_sc, sin_sc, *, q_scale):
    j = pl.program_id(1)
    tm = x_ref.shape[0]

    @pl.when(j == 0)
    def _():
        h = _rms(x_ref[...], g_ref[...]) * (1.0 + mod_ref[1:2, :]) + mod_ref[0:1, :]
        h_sc[...] = h.astype(BF16)
        ang = pos_ref[...].astype(F32) * invf_ref[...]
        cos_sc[...] = jnp.cos(ang)
        sin_sc[...] = jnp.sin(ang) * sgn_ref[...]

    acc = jnp.dot(h_sc[...], w_ref[...], preferred_element_type=F32)
    is_rope = (j == 1) | (j == 2)

    @pl.when(jnp.logical_not(is_rope))
    def _():
        o_ref[...] = acc

    @pl.when(is_rope)
    def _():
        scale = jnp.where(j == 1, q_scale, 1.0).astype(F32)
        cs = cos_sc[...] * scale
        sn = sin_sc[...] * scale
        lane = lax.broadcasted_iota(I32, (tm, LANES), 1)
        half = ROPE_DIM // 2
        for hb in range(N_HEADS):
            t = acc[:, hb * HEAD_DIM:(hb + 1) * HEAD_DIM]
            rot = jnp.where(lane < half, pltpu.roll(t, LANES - half, 1), pltpu.roll(t, half, 1))
            o_ref[:, hb * HEAD_DIM:(hb + 1) * HEAD_DIM] = t * cs + rot * sn


def _inproj(x2, mod3, g_mix, pos2, w_in_bf, seq):
    t, d = x2.shape
    n = w_in_bf.shape[1]
    tm, tn = 512, 1024
    assert tn == FOURIER_W == ATTN_W and HEAD_DIM == LANES
    half = ROPE_DIM // 2
    inv_freq = ROPE_THETA ** (-jnp.arange(0, ROPE_DIM, 2, dtype=F32) / ROPE_DIM)
    invf = jnp.concatenate([inv_freq, inv_freq, jnp.zeros((LANES - ROPE_DIM,), F32)]).reshape(1, LANES)
    sgn = jnp.concatenate([-jnp.ones((half,), F32), jnp.ones((half,), F32),
                           jnp.zeros((LANES - ROPE_DIM,), F32)]).reshape(1, LANES)
    tpb = seq // tm
    return pl.pallas_call(
        functools.partial(_inproj_body, q_scale=HEAD_DIM ** -0.5),
        out_shape=_SDS((t, n), F32),
        grid=(t // tm, n // tn),
        in_specs=[pl.BlockSpec((tm, d), lambda i, j: (i, 0)),
                  pl.BlockSpec((None, 6, d), lambda i, j: (i // tpb, 0, 0)),
                  pl.BlockSpec((1, d), lambda i, j: (0, 0)),
                  pl.BlockSpec((tm, 1), lambda i, j: (i, 0)),
                  pl.BlockSpec((1, LANES), lambda i, j: (0, 0)),
                  pl.BlockSpec((1, LANES), lambda i, j: (0, 0)),
                  pl.BlockSpec((d, tn), lambda i, j: (0, j))],
        out_specs=pl.BlockSpec((tm, tn), lambda i, j: (i, j)),
        scratch_shapes=[pltpu.VMEM((tm, d), BF16), pltpu.VMEM((tm, LANES), F32), pltpu.VMEM((tm, LANES), F32)],
        compiler_params=_params(2),
        name="inproj",
    )(x2, mod3, g_mix.reshape(1, d), pos2, invf, sgn, w_in_bf)


def _dft_tables(n):
    idx = jnp.arange(n, dtype=I32)
    ang = ((idx[:, None] * idx[None, :]) % n).astype(F32) * (2.0 * math.pi / n)
    return jnp.cos(ang), jnp.sin(ang)


def _dft_tables_split(n, blk=64):
    j = jnp.arange(n, dtype=I32)[:, None]
    b = jnp.arange(blk, dtype=I32)[None, :]
    q = jnp.arange(n // blk, dtype=I32)[None, :] * blk
    ab = ((j * b) % n).astype(F32) * (2.0 * math.pi / n)
    aq = ((j * q) % n).astype(F32) * (2.0 * math.pi / n)
    cb, sb, cq, sq = jnp.cos(ab)[:, None, :], jnp.sin(ab)[:, None, :], jnp.cos(aq)[:, :, None], jnp.sin(aq)[:, :, None]
    return (cq * cb - sq * sb).reshape(n, n), (sq * cb + cq * sb).reshape(n, n)


def _chdft_body(p_ref, cs_ref, xc_ref, xs_ref):
    gw = FOURIER_GROUP_W
    for g in range(N_FOURIER_GROUPS):
        f = p_ref[:, g * gw:(g + 1) * gw].astype(BF16)
        z = jnp.dot(f, cs_ref[...], preferred_element_type=F32)
        xc_ref[:, g * gw:(g + 1) * gw] = z[:, :gw].astype(BF16)
        xs_ref[:, g * gw:(g + 1) * gw] = z[:, gw:].astype(BF16)


def _chdft(proj, cs_ch):
    t = proj.shape[0]
    tm = 1024
    return pl.pallas_call(
        _chdft_body,
        out_shape=(_SDS((t, FOURIER_W), BF16), _SDS((t, FOURIER_W), BF16)),
        grid=(t // tm,),
        in_specs=[pl.BlockSpec((tm, FOURIER_W), lambda i: (i, 0)),
                  pl.BlockSpec(cs_ch.shape, lambda i: (0, 0))],
        out_specs=(pl.BlockSpec((tm, FOURIER_W), lambda i: (i, 0)),
                   pl.BlockSpec((tm, FOURIER_W), lambda i: (i, 0))),
        compiler_params=_params(1),
        name="chdft",
    )(proj, cs_ch)


def _seqdft_body(c_ref, ns_ref, xc_ref, xs_ref, g_ref, o_ref, acc, *, scale):
    k = pl.program_id(2)

    @pl.when(k == 0)
    def _():
        acc[...] = jnp.zeros_like(acc)

    acc[...] += (jnp.dot(c_ref[...], xc_ref[...], preferred_element_type=F32)
                 + jnp.dot(ns_ref[...], xs_ref[...], preferred_element_type=F32))

    @pl.when(k == pl.num_programs(2) - 1)
    def _():
        o_ref[...] = _rms(acc[...] * scale, g_ref[...]).astype(BF16)


def _seqdft(xc3, xs3, c_seq, ns_seq, g_fourier):
    b, s, w = xc3.shape
    tm = tk = 1024
    return pl.pallas_call(
        functools.partial(_seqdft_body, scale=1.0 / math.sqrt(s * FOURIER_GROUP_W)),
        out_shape=_SDS((b, s, w), BF16),
        grid=(b, s // tm, s // tk),
        in_specs=[pl.BlockSpec((tm, tk), lambda bi, i, k: (i, k)),
                  pl.BlockSpec((tm, tk), lambda bi, i, k: (i, k)),
                  pl.BlockSpec((None, tk, w), lambda bi, i, k: (bi, k, 0)),
                  pl.BlockSpec((None, tk, w), lambda bi, i, k: (bi, k, 0)),
                  pl.BlockSpec((1, w), lambda bi, i, k: (0, 0))],
        out_specs=pl.BlockSpec((None, tm, w), lambda bi, i, k: (bi, i, 0)),
        scratch_shapes=[pltpu.VMEM((tm, w), F32)],
        compiler_params=_params(3),
        name="seqdft",
    )(c_seq, ns_seq, xc3, xs3, g_fourier.reshape(1, w))


def _attn_body(q_ref, k_ref, v_ref, o_ref, kpad, vpad, ob0, ob1, ob2, lb0, lb1, lb2):
    s = q_ref.shape[0]
    obufs, lbufs = (ob0, ob1, ob2), (lb0, lb1, lb2)
    qb, span, half, pad = ATTN_QB, ATTN_SPAN, ATTN_HALF, ATTN_PAD
    chunk = 512

    zeros = jnp.zeros((pad, HEAD_DIM), F32)
    for buf in (kpad, vpad):
        buf[0:pad, :] = zeros
        buf[pad + s:pad + s + pad, :] = zeros

    def copy_in(c, carry):
        r0 = pl.multiple_of(c * chunk, chunk)
        kpad[pl.ds(pad + r0, chunk), :] = k_ref[pl.ds(r0, chunk), :]
        vpad[pl.ds(pad + r0, chunk), :] = v_ref[pl.ds(r0, chunk), :]
        return carry

    lax.fori_loop(0, s // chunk, copy_in, 0)

    row = lax.broadcasted_iota(I32, (qb, span), 0)
    col = lax.broadcasted_iota(I32, (qb, span), 1)
    band = (col >= row) & (col <= row + 2 * half)

    for p, (window, dil) in enumerate(DILATED_PATTERNS):
        assert window // (2 * dil) == half
        length = s // dil
        nblk = length // qb
        shift = nblk.bit_length() - 1
        assert nblk == 1 << shift
        obuf, lbuf = obufs[p], lbufs[p]

        def rows(start, size):
            return pl.ds(start, size) if dil == 1 else pl.ds(start, size, stride=dil)

        def block(i, carry):
            r = i >> shift
            n = i & (nblk - 1)
            q0 = r + n * (qb * dil)
            k0 = pad + r + (n * qb - half) * dil
            q = q_ref[rows(q0, qb), :].astype(BF16)
            ks = kpad[rows(k0, span), :].astype(BF16)
            vs = vpad[rows(k0, span), :].astype(BF16)
            sc = lax.dot_general(q, ks, (((1,), (1,)), ((), ())), preferred_element_type=F32)
            lo = half - n * qb
            valid = band & (col >= lo) & (col < lo + length)
            sc = jnp.where(valid, sc, -jnp.inf)
            m = jnp.max(sc, axis=-1, keepdims=True)
            e = jnp.exp(sc - m)
            den = jnp.sum(e, axis=-1, keepdims=True)
            o = jnp.dot(e.astype(BF16), vs, preferred_element_type=F32) / den
            obuf[rows(q0, qb), :] = o
            lbuf[rows(q0, qb), :] = jnp.broadcast_to(m + jnp.log(den), (qb, HEAD_DIM))
            return carry

        lax.fori_loop(0, dil * nblk, block, 0, unroll=2)

    def mix(c, carry):
        r0 = pl.multiple_of(c * chunk, chunk)
        sl = pl.ds(r0, chunk)
        l0, l1, l2 = lb0[sl, :], lb1[sl, :], lb2[sl, :]
        m = jnp.maximum(jnp.maximum(l0, l1), l2)
        e0, e1, e2 = jnp.exp(l0 - m), jnp.exp(l1 - m), jnp.exp(l2 - m)
        o_ref[sl, :] = (e0 * ob0[sl, :] + e1 * ob1[sl, :] + e2 * ob2[sl, :]) / (e0 + e1 + e2)
        return carry

    lax.fori_loop(0, s // chunk, mix, 0)


def _attention(proj3):
    b, s, _ = proj3.shape
    qoff, koff, voff = (FOURIER_W // HEAD_DIM, (FOURIER_W + ATTN_W) // HEAD_DIM,
                        (FOURIER_W + 2 * ATTN_W) // HEAD_DIM)
    head = lambda off: pl.BlockSpec((None, s, HEAD_DIM), lambda bi, h: (bi, 0, off + h))
    full = pltpu.VMEM((s, HEAD_DIM), F32)
    padded = pltpu.VMEM((s + 2 * ATTN_PAD, HEAD_DIM), F32)
    return pl.pallas_call(
        _attn_body,
        out_shape=_SDS((b, s, ATTN_W), F32),
        grid=(b, N_HEADS),
        in_specs=[head(qoff), head(koff), head(voff)],
        out_specs=pl.BlockSpec((None, s, HEAD_DIM), lambda bi, h: (bi, 0, h)),
        scratch_shapes=[padded, padded, full, full, full, full, full, full],
        compiler_params=_params(2),
        name="attn",
    )(proj3, proj3, proj3)


def _outproj_body(x_ref, f_ref, a_ref, mod_ref, ga_ref, gf_ref, wo_ref, wrh_ref, wrl_ref,
                  x1_ref, h2_ref, lg_ref):
    an = _rms(a_ref[...], ga_ref[...]).astype(BF16)
    y = (jnp.dot(f_ref[...], wo_ref[0:FOURIER_W, :], preferred_element_type=F32)
         + jnp.dot(an, wo_ref[FOURIER_W:FOURIER_W + ATTN_W, :], preferred_element_type=F32))
    x1 = x_ref[...] + mod_ref[2:3, :] * y
    x1_ref[...] = x1
    h2 = _rms(x1, gf_ref[...]) * (1.0 + mod_ref[4:5, :]) + mod_ref[3:4, :]
    _store_row_tiles(h2_ref, h2)
    hi = h2.astype(BF16)
    lo = (h2 - hi.astype(F32)).astype(BF16)
    lg_ref[...] = (jnp.dot(hi, wrh_ref[...], preferred_element_type=F32)
                   + jnp.dot(lo, wrh_ref[...], preferred_element_type=F32)
                   + jnp.dot(hi, wrl_ref[...], preferred_element_type=F32))


def _outproj(x2, f_out, a_raw, mod3, g_attn, g_ffn, w_out_bf, wr_hi, wr_lo, seq):
    t, d = x2.shape
    tm = 256
    tpb = seq // tm
    row = lambda w: pl.BlockSpec((tm, w), lambda i: (i, 0))
    const = lambda shape: pl.BlockSpec(shape, lambda i: (0,) * len(shape))
    return pl.pallas_call(
        _outproj_body,
        out_shape=(_SDS((t, d), F32), _SDS((t * ROW_SUB, LANES), U32), _SDS((t, N_EXPERTS), F32)),
        grid=(t // tm,),
        in_specs=[row(d), row(FOURIER_W), row(ATTN_W),
                  pl.BlockSpec((None, 6, d), lambda i: (i // tpb, 0, 0)),
                  const((1, ATTN_W)), const((1, d)), const(w_out_bf.shape),
                  const(wr_hi.shape), const(wr_lo.shape)],
        out_specs=(row(d), pl.BlockSpec((tm * ROW_SUB, LANES), lambda i: (i, 0)), row(N_EXPERTS)),
        compiler_params=_params(1),
        name="outproj",
    )(x2, f_out, a_raw, mod3, g_attn.reshape(1, ATTN_W), g_ffn.reshape(1, d), w_out_bf, wr_hi, wr_lo)


def _route_body(lg_ref, b_ref, sel_ref, w_ref, rank_ref, cnt_ref, carry):
    i = pl.program_id(0)
    tm = lg_ref.shape[0]
    ne = N_EXPERTS

    @pl.when(i == 0)
    def _():
        carry[...] = jnp.zeros_like(carry)

    scores = jax.nn.sigmoid(lg_ref[...].T)
    biased = scores + b_ref[...]
    groups = [biased[g * GROUP_SIZE:(g + 1) * GROUP_SIZE, :] for g in range(N_GROUPS)]
    gscore = []
    for blk in groups:
        m1 = jnp.max(blk, axis=0, keepdims=True)
        is1 = blk == m1
        n1 = jnp.sum(is1.astype(F32), axis=0, keepdims=True)
        m2 = jnp.max(jnp.where(is1, -jnp.inf, blk), axis=0, keepdims=True)
        gscore.append(m1 + jnp.where(n1 >= 2.0, m1, m2))
    parts = []
    for g in range(N_GROUPS):
        beaten = jnp.zeros((1, tm), F32)
        for gp in range(N_GROUPS):
            if gp != g:
                wins = (gscore[gp] >= gscore[g]) if gp < g else (gscore[gp] > gscore[g])
                beaten += wins.astype(F32)
        parts.append(jnp.where(beaten < float(TOPK_GROUP), groups[g], -jnp.inf))
    masked = jnp.concatenate(parts, axis=0)

    eidx = lax.broadcasted_iota(I32, (ne, tm), 0).astype(F32)
    chosen = jnp.zeros((ne, tm), F32)
    sels, ws = [], []
    for _ in range(TOP_K):
        m = jnp.max(masked, axis=0, keepdims=True)
        idx = jnp.min(jnp.where(masked == m, eidx, float(ne)), axis=0, keepdims=True)
        hit = eidx == idx
        sels.append(idx)
        ws.append(jnp.sum(jnp.where(hit, scores, 0.0), axis=0, keepdims=True))
        masked = jnp.where(hit, -jnp.inf, masked)
        chosen = jnp.where(hit, 1.0, chosen)
    wsum = functools.reduce(lambda a, b: a + b, ws)

    upper = (lax.broadcasted_iota(I32, (tm, tm), 0) <= lax.broadcasted_iota(I32, (tm, tm), 1)).astype(BF16)
    incl = jnp.dot(chosen.astype(BF16), upper, preferred_element_type=F32)
    rank_all = carry[...] + incl - chosen
    for k in range(TOP_K):
        sel_ref[k:k + 1, :] = sels[k].astype(I32)
        w_ref[k:k + 1, :] = ws[k] / wsum * ROUTED_SCALE
        rk = jnp.sum(jnp.where(eidx == sels[k], rank_all, 0.0), axis=0, keepdims=True)
        rank_ref[k:k + 1, :] = rk.astype(I32)
    carry[...] = carry[...] + jnp.sum(chosen, axis=1, keepdims=True)
    cnt_ref[...] = carry[...]


def _route(logits, b_router):
    t, ne = logits.shape
    tm = 512
    tile = lambda dt: pl.BlockSpec((TOP_K, tm), lambda i: (0, i))
    return pl.pallas_call(
        _route_body,
        out_shape=(_SDS((TOP_K, t), I32), _SDS((TOP_K, t), F32), _SDS((TOP_K, t), I32), _SDS((ne, 1), F32)),
        grid=(t // tm,),
        in_specs=[pl.BlockSpec((tm, ne), lambda i: (i, 0)), pl.BlockSpec((ne, 1), lambda i: (0, 0))],
        out_specs=(tile(I32), tile(F32), tile(I32), pl.BlockSpec((ne, 1), lambda i: (0, 0))),
        scratch_shapes=[pltpu.VMEM((ne, 1), F32)],
        compiler_params=_params(1),
        name="route",
    )(logits, b_router.reshape(ne, 1))


def _dest_body(cnt_ref, sel_ref, rank_ref, dest_ref, be_ref, first_ref, nblk_ref, nused_ref, *, n_blocks):
    ne = N_EXPERTS
    tm = sel_ref.shape[1]
    g = EXPERT_ROWS
    cnt = cnt_ref[...].astype(I32)
    assert g & (g - 1) == 0
    nb = ((cnt + (g - 1)) >> (g.bit_length() - 1)).astype(F32)
    lower = (lax.broadcasted_iota(I32, (ne, ne), 1) < lax.broadcasted_iota(I32, (ne, ne), 0)).astype(BF16)
    first = jnp.dot(lower, jnp.broadcast_to(nb, (ne, LANES)).astype(BF16),
                    preferred_element_type=F32)[:, 0:1]
    row0 = first * float(g)
    eidx = lax.broadcasted_iota(I32, (ne, tm), 0)
    for k in range(TOP_K):
        base = jnp.sum(jnp.where(eidx == sel_ref[k:k + 1, :], row0, 0.0), axis=0, keepdims=True)
        dest_ref[k:k + 1, :] = base.astype(I32) + rank_ref[k:k + 1, :]
    blk = lax.broadcasted_iota(I32, (ne, n_blocks), 1).astype(F32)
    owner = lax.broadcasted_iota(I32, (ne, n_blocks), 0).astype(F32)
    inside = (blk >= first) & (blk < first + nb)
    be_ref[...] = jnp.sum(jnp.where(inside, owner, 0.0), axis=0, keepdims=True).astype(I32)
    first_ref[...] = first.astype(I32)
    nblk_ref[...] = nb.astype(I32)
    nused_ref[...] = jnp.broadcast_to(jnp.sum(nb, axis=0, keepdims=True), (1, LANES)).astype(I32)


def _dest(counts, sel, rank, n_blocks):
    t = sel.shape[1]
    ne = N_EXPERTS
    tm = 2048
    tile = pl.BlockSpec((TOP_K, tm), lambda i: (0, i))
    const = lambda shape: pl.BlockSpec(shape, lambda i: (0, 0))
    return pl.pallas_call(
        functools.partial(_dest_body, n_blocks=n_blocks),
        out_shape=(_SDS((TOP_K, t), I32), _SDS((1, n_blocks), I32), _SDS((ne, 1), I32),
                   _SDS((ne, 1), I32), _SDS((1, LANES), I32)),
        grid=(t // tm,),
        in_specs=[const((ne, 1)), tile, tile],
        out_specs=(tile, const((1, n_blocks)), const((ne, 1)), const((ne, 1)), const((1, LANES))),
        compiler_params=_params(1),
        name="dest",
    )(counts, sel, rank)


def _tile_rows(row, n):
    return pl.ds(pl.multiple_of(row * ROW_SUB, ROW_SUB), n * ROW_SUB)


def _dispatch_body(first_sm, nblk_sm, cnt_sm, dest_hbm, h_ref, xs_hbm, dsm, zbuf, sem_i, sem_r, sem_z):
    i = pl.program_id(0)
    tm = h_ref.shape[0] // ROW_SUB
    g = EXPERT_ROWS
    idx_copy = pltpu.make_async_copy(dest_hbm.at[i], dsm, sem_i)
    idx_copy.start()

    @pl.when(i == 0)
    def _():
        zbuf[...] = jnp.zeros_like(zbuf)

        def per_expert(wait):
            def body(e, carry):
                rows = nblk_sm[e] * g
                pad = rows - cnt_sm[e]
                pos = first_sm[e] * g + cnt_sm[e]
                n = g // 2
                while n >= 1:
                    take = (pad & n) != 0

                    @pl.when(take)
                    def _(pos=pos, n=n):
                        cp = pltpu.make_async_copy(zbuf.at[_tile_rows(0, n)], xs_hbm.at[_tile_rows(pos, n)], sem_z)
                        cp.wait() if wait else cp.start()

                    pos = pos + jnp.where(take, n, 0)
                    n //= 2
                return carry
            lax.fori_loop(0, N_EXPERTS, body, 0)

        per_expert(False)
        per_expert(True)

    idx_copy.wait()

    def scatter(tk, carry):
        src = h_ref.at[_tile_rows(tk, 1)]
        for k in range(TOP_K):
            pltpu.make_async_copy(src, xs_hbm.at[_tile_rows(dsm[k * tm + tk], 1)], sem_r).start(priority=k % 2)
        return carry

    lax.fori_loop(0, tm, scatter, 0)
    for k in range(TOP_K):
        pltpu.make_async_copy(h_ref, xs_hbm.at[_tile_rows(0, tm)], sem_r).wait()


def _dispatch(first, nblk, cnt, dest_tiles, h2p, n_rows):
    n_tiles, per_tile = dest_tiles.shape
    tm = per_tile // TOP_K
    return pl.pallas_call(
        _dispatch_body,
        out_shape=_SDS((n_rows * ROW_SUB, LANES), U32),
        grid_spec=pltpu.PrefetchScalarGridSpec(
            num_scalar_prefetch=3,
            grid=(n_tiles,),
            in_specs=[pl.BlockSpec(memory_space=pl.ANY),
                      pl.BlockSpec((tm * ROW_SUB, LANES), lambda i, *_: (i, 0))],
            out_specs=pl.BlockSpec(memory_space=pl.ANY),
            scratch_shapes=[pltpu.SMEM((per_tile,), I32), pltpu.VMEM((EXPERT_ROWS // 2 * ROW_SUB, LANES), U32),
                            pltpu.SemaphoreType.DMA, pltpu.SemaphoreType.DMA, pltpu.SemaphoreType.DMA]),
        compiler_params=_params(1),
        name="dispatch",
    )(first, nblk, cnt, dest_tiles, h2p)


def _experts_body(be_sm, nblk_sm, nused_sm, x_ref, wg_hbm, wu_hbm, wd_hbm, y_ref,
                  wg_st, wu_st, wd_st, wg_bf, wu_bf, wd_bf, slot_sm, sems):
    i = pl.program_id(0)
    g = EXPERT_ROWS
    nused = nused_sm[0]
    used = i < nused
    e = be_sm[i]
    fresh = used & ((i == 0) | (e != be_sm[jnp.maximum(i - 1, 0)]))

    def weight_copies(expert, slot):
        return [pltpu.make_async_copy(src.at[expert], dst.at[slot], sems.at[slot])
                for src, dst in ((wg_hbm, wg_st), (wu_hbm, wu_st), (wd_hbm, wd_st))]

    @pl.when(i == 0)
    def _():
        slot_sm[0] = 0
        for cp in weight_copies(e, 0):
            cp.start()

    @pl.when(fresh)
    def _():
        slot = slot_sm[0]
        nxt = i + nblk_sm[e]

        @pl.when(nxt < nused)
        def _():
            for cp in weight_copies(be_sm[nxt], 1 - slot):
                cp.start()

        for cp in weight_copies(e, slot):
            cp.wait()
        wg_bf[...] = wg_st[slot].astype(BF16)
        wu_bf[...] = wu_st[slot].astype(BF16)
        wd_bf[...] = wd_st[slot].astype(BF16)
        slot_sm[0] = 1 - slot

    @pl.when(used)
    def _():
        x_lo, x_hi = _load_row_tiles_bf16(x_ref, g)
        a = (jnp.dot(x_lo, wg_bf[0:ROW_W, :], preferred_element_type=F32)
             + jnp.dot(x_hi, wg_bf[ROW_W:2 * ROW_W, :], preferred_element_type=F32))
        u = (jnp.dot(x_lo, wu_bf[0:ROW_W, :], preferred_element_type=F32)
             + jnp.dot(x_hi, wu_bf[ROW_W:2 * ROW_W, :], preferred_element_type=F32))
        hmid = (a * jax.nn.sigmoid(a) * u).astype(BF16)
        _store_row_tiles(y_ref, jnp.dot(hmid, wd_bf[...], preferred_element_type=F32))


def _experts(be, nblk, nused, xs, w_gate, w_up, w_down):
    g = EXPERT_ROWS
    n_blocks = xs.shape[0] // (g * ROW_SUB)
    d = w_gate.shape[1]
    rows = pl.BlockSpec((g * ROW_SUB, LANES), lambda i, be, nb, nu: (jnp.minimum(i, nu[0] - 1), 0))
    hbm = pl.BlockSpec(memory_space=pl.ANY)
    return pl.pallas_call(
        _experts_body,
        out_shape=_SDS(xs.shape, U32),
        grid_spec=pltpu.PrefetchScalarGridSpec(
            num_scalar_prefetch=3,
            grid=(n_blocks,),
            in_specs=[rows, hbm, hbm, hbm],
            out_specs=rows,
            scratch_shapes=[pltpu.VMEM((2, d, D_EXPERT), F32), pltpu.VMEM((2, d, D_EXPERT), F32),
                            pltpu.VMEM((2, D_EXPERT, d), F32),
                            pltpu.VMEM((d, D_EXPERT), BF16), pltpu.VMEM((d, D_EXPERT), BF16),
                            pltpu.VMEM((D_EXPERT, d), BF16),
                            pltpu.SMEM((1,), I32), pltpu.SemaphoreType.DMA((2,))]),
        compiler_params=_params(1),
        name="experts",
    )(be, nblk, nused, xs, w_gate, w_up, w_down)


def _combine_body(dest_hbm, ys_hbm, x1_ref, h_ref, w_ref, mod_ref, gfin_ref, sg_ref, su_ref, sd_ref, o_ref,
                  dsm, gbuf, tot, sem_i, sem_r):
    i = pl.program_id(0)
    tm = x1_ref.shape[0]
    idx_copy = pltpu.make_async_copy(dest_hbm.at[i], dsm, sem_i)
    idx_copy.start()
    idx_copy.wait()

    def gather(tk, carry):
        for k in range(TOP_K):
            pltpu.make_async_copy(ys_hbm.at[_tile_rows(dsm[k * tm + tk], 1)],
                                  gbuf.at[k, _tile_rows(tk, 1)], sem_r).start(priority=k % 2)
        return carry

    lax.fori_loop(0, tm, gather, 0)

    h_lo, h_hi = _load_row_tiles_bf16(h_ref, tm)
    a = (jnp.dot(h_lo, sg_ref[0:ROW_W, :], preferred_element_type=F32)
         + jnp.dot(h_hi, sg_ref[ROW_W:2 * ROW_W, :], preferred_element_type=F32))
    u = (jnp.dot(h_lo, su_ref[0:ROW_W, :], preferred_element_type=F32)
         + jnp.dot(h_hi, su_ref[ROW_W:2 * ROW_W, :], preferred_element_type=F32))
    tot[...] = jnp.dot((a * jax.nn.sigmoid(a) * u).astype(BF16), sd_ref[...], preferred_element_type=F32)

    for k in range(TOP_K):
        pltpu.make_async_copy(ys_hbm.at[_tile_rows(0, tm)], gbuf.at[k], sem_r).wait()
    for k in range(TOP_K):
        wk = jnp.broadcast_to(w_ref[:, k:k + 1], (tm, LANES))
        for s in range(ROW_SUB):
            lo, hi = _load_row_tile_chunk(gbuf.at[k], tm, s)
            tot[:, s * LANES:(s + 1) * LANES] += lo * wk
            tot[:, ROW_W + s * LANES:ROW_W + (s + 1) * LANES] += hi * wk
    o_ref[...] = _rms(x1_ref[...] + mod_ref[5:6, :] * tot[...], gfin_ref[...])


def _combine(dest_tiles, ys, x1, h2p, w_tok, mod3, g_final, ws_gate, ws_up, ws_down, seq):
    t, d = x1.shape
    n_tiles, per_tile = dest_tiles.shape
    tm = per_tile // TOP_K
    tpb = seq // tm
    row = lambda w: pl.BlockSpec((tm, w), lambda i: (i, 0))
    const = lambda shape: pl.BlockSpec(shape, lambda i: (0,) * len(shape))
    return pl.pallas_call(
        _combine_body,
        out_shape=_SDS((t, d), F32),
        grid=(n_tiles,),
        in_specs=[pl.BlockSpec(memory_space=pl.ANY), pl.BlockSpec(memory_space=pl.ANY),
                  row(d), pl.BlockSpec((tm * ROW_SUB, LANES), lambda i: (i, 0)), row(TOP_K),
                  pl.BlockSpec((None, 6, d), lambda i: (i // tpb, 0, 0)),
                  const((1, d)), const(ws_gate.shape), const(ws_up.shape), const(ws_down.shape)],
        out_specs=row(d),
        scratch_shapes=[pltpu.SMEM((per_tile,), I32), pltpu.VMEM((TOP_K, tm * ROW_SUB, LANES), U32),
                        pltpu.VMEM((tm, d), F32), pltpu.SemaphoreType.DMA, pltpu.SemaphoreType.DMA],
        compiler_params=_params(1),
        name="combine",
    )(dest_tiles, ys, x1, h2p, w_tok, mod3, g_final.reshape(1, d), ws_gate, ws_up, ws_down)


def _tile_major(a, tm):
    k, t = a.shape
    return a.reshape(k, t // tm, tm).transpose(1, 0, 2).reshape(t // tm, k * tm)


def kernel(x, c, positions, w_ada, b_ada, g_mix, g_ffn, g_final, w_in, g_fourier_out, g_attn_out, w_out,
           w_router, b_router, w_gate, w_up, w_down, w_shared_gate, w_shared_up, w_shared_down):
    b, s, d = x.shape
    t = b * s
    x2 = x.reshape(t, d)
    mod3 = _adaln(c, w_ada, b_ada).reshape(b, 6, d)

    proj = _inproj(x2, mod3, g_mix, positions.reshape(t, 1), w_in.astype(BF16), s)

    cc, sc = _dft_tables(FOURIER_GROUP_W)
    xc, xs = _chdft(proj, jnp.concatenate([cc, sc], axis=1).astype(BF16))
    cseq, sseq = _dft_tables_split(s)
    f_out = _seqdft(xc.reshape(b, s, FOURIER_W), xs.reshape(b, s, FOURIER_W),
                    cseq.astype(BF16), (-sseq).astype(BF16), g_fourier_out)

    a_raw = _attention(proj.reshape(b, s, proj.shape[1]))

    wr_hi = w_router.astype(BF16)
    wr_lo = (w_router - wr_hi.astype(F32)).astype(BF16)
    assert d == 2 * ROW_W
    x1, h2p, logits = _outproj(x2, f_out.reshape(t, FOURIER_W), a_raw.reshape(t, ATTN_W), mod3,
                               g_attn_out, g_ffn, w_out.astype(BF16), wr_hi, wr_lo, s)

    sel, w_sel, rank, counts = _route(logits, b_router)
    n_blocks = (t * TOP_K) // EXPERT_ROWS + N_EXPERTS
    dest, be, first, nblk, nused = _dest(counts, sel, rank, n_blocks)

    tm_rows = 256
    dest_tiles = _tile_major(dest, tm_rows)
    nblk1 = nblk.reshape(-1)
    xs_sorted = _dispatch(first.reshape(-1), nblk1, counts.astype(I32).reshape(-1),
                          dest_tiles, h2p, n_blocks * EXPERT_ROWS)
    ys = _experts(be.reshape(-1), nblk1, nused[0, :1], xs_sorted, w_gate, w_up, w_down)
    out = _combine(dest_tiles, ys, x1, h2p, w_sel.T, mod3, g_final,
                   w_shared_gate.astype(BF16), w_shared_up.astype(BF16), w_shared_down.astype(BF16), s)
    return out.reshape(b, s, d)
```
